```python
import numpy as np
import jax
import jax.numpy as jnp
from jax import lax

D_MODEL = 2048
BATCH = 4
SEQ = 2048
DEPTH = 4

GRID_W = 64
CTX_LEN = 256
HEAD_DIM = 128
N_HEADS_A = D_MODEL // (2 * HEAD_DIM)
N_HEADS_B = D_MODEL // (2 * HEAD_DIM)
WIDTH_A = N_HEADS_A * HEAD_DIM
WIDTH_B = N_HEADS_B * HEAD_DIM
N_GATES = 4 * N_HEADS_B
SPLIT_EVEN = (WIDTH_A, 2 * WIDTH_A, 3 * WIDTH_A,
              3 * WIDTH_A + WIDTH_B, 3 * WIDTH_A + 2 * WIDTH_B,
              3 * WIDTH_A + 3 * WIDTH_B, 3 * WIDTH_A + 4 * WIDTH_B)
IN_EVEN = 3 * WIDTH_A + 4 * WIDTH_B + N_GATES
WIN_ROWS = 8
WIN_COLS = 16
Q_COL_BLOCK = 16
K_COL_BLOCK = 32
ATT_SCALE = HEAD_DIM ** -0.5
MLSTM_CHUNK = 64
MLSTM_CONV = 3
SGU_CHUNK = 128
SGU_GROUPS = 8
SGU_WIDTH = 2 * D_MODEL
N_GROUPS = 4
EXPERTS_PER_GROUP = 8
N_EXPERTS = N_GROUPS * EXPERTS_PER_GROUP
TOP_K_IN_GROUP = 2
D_EXPERT = D_MODEL // 4
N_EVEN = (DEPTH + 1) // 2
N_ODD = DEPTH // 2
EPS = 1e-6
NEG = -1e30
F32 = jnp.float32

kernel_name = "hybrid_natten_mlstm_sgu_hmoe_diffusion"


def rms_norm(x, gain):
    xf = x.astype(F32)
    y = xf * lax.rsqrt(jnp.mean(xf * xf, axis=-1, keepdims=True) + EPS)
    return (y * gain.astype(F32)).astype(x.dtype)


def modulate(x, gain, shift, scale):
    return rms_norm(x, gain) * (1 + scale) + shift


def split_heads(t, n_heads):
    b, t_len, _ = t.shape
    return t.reshape(b, t_len, n_heads, -1).transpose(0, 2, 1, 3)


def merge_heads(t):
    b, h, t_len, d = t.shape
    return t.transpose(0, 2, 1, 3).reshape(b, t_len, h * d)


def centred_dwconv(x, w):
    k = w.shape[0]
    pad = k // 2
    return lax.conv_general_dilated(
        x, w.astype(x.dtype)[:, None, :], window_strides=(1,), padding=[(pad, k - 1 - pad)],
        dimension_numbers=('NWC', 'WIO', 'NWC'), feature_group_count=x.shape[-1])


def neighbourhood_attention(q, k, v, k_ctx, v_ctx, rpb):
    b, h, n, dh = q.shape
    rows = n // GRID_W
    kr = min(WIN_ROWS, rows)
    nb = GRID_W // Q_COL_BLOCK
    q_col = np.arange(GRID_W).reshape(nb, Q_COL_BLOCK)
    k_col = (np.clip(np.arange(nb) * Q_COL_BLOCK - WIN_COLS // 2, 0, GRID_W - K_COL_BLOCK)[:, None]
             + np.arange(K_COL_BLOCK))
    w_start = np.clip(q_col - WIN_COLS // 2, 0, GRID_W - WIN_COLS)
    rel = k_col[:, None, :] - w_start[:, :, None]
    in_win = (rel >= 0) & (rel < WIN_COLS)
    dc_idx = np.clip(k_col[:, None, :] - q_col[:, :, None] + WIN_COLS - 1, 0, 2 * WIN_COLS - 2)
    n_loc = kr * K_COL_BLOCK
    mask = np.broadcast_to(in_win[:, :, None, :], (nb, Q_COL_BLOCK, kr, K_COL_BLOCK)).reshape(
        nb, Q_COL_BLOCK, n_loc)
    k_grid = k.reshape(b, h, rows, GRID_W, dh)
    v_grid = v.reshape(b, h, rows, GRID_W, dh)
    q_rows = jnp.moveaxis(q.reshape(b, h, rows, nb, Q_COL_BLOCK, dh), 2, 0)

    def attend_row(args):
        r, q_r = args
        r0 = jnp.clip(r - kr // 2, 0, rows - kr)
        k_blk = lax.dynamic_slice_in_dim(k_grid, r0, kr, axis=2)[:, :, :, k_col]
        v_blk = lax.dynamic_slice_in_dim(v_grid, r0, kr, axis=2)[:, :, :, k_col]
        k_blk = jnp.moveaxis(k_blk, 3, 2).reshape(b, h, nb, n_loc, dh)
        v_blk = jnp.moveaxis(v_blk, 3, 2).reshape(b, h, nb, n_loc, dh)
        dr_idx = r0 + jnp.arange(kr) - r + WIN_ROWS - 1
        bias = rpb[:, dr_idx][:, :, dc_idx]
        bias = jnp.transpose(bias, (0, 2, 3, 1, 4)).reshape(h, nb, Q_COL_BLOCK, n_loc)
        s_loc = jnp.einsum('bhnqd,bhnkd->bhnqk', q_r, k_blk).astype(F32) + bias.astype(F32)
        s_loc = jnp.where(mask, s_loc, NEG)
        s_ctx = jnp.einsum('bhnqd,bhkd->bhnqk', q_r, k_ctx).astype(F32)
        p = jax.nn.softmax(jnp.concatenate([s_loc, s_ctx], axis=-1), axis=-1).astype(v.dtype)
        return (jnp.einsum('bhnqk,bhnkd->bhnqd', p[..., :n_loc], v_blk)
                + jnp.einsum('bhnqk,bhkd->bhnqd', p[..., n_loc:], v_ctx))

    out = lax.map(attend_row, (jnp.arange(rows), q_rows))
    return jnp.moveaxis(out, 0, 2).reshape(b, h, n, dh)


def context_attention(q, k, v):
    p = jax.nn.softmax(jnp.einsum('bhqd,bhkd->bhqk', q, k).astype(F32), axis=-1).astype(v.dtype)
    return jnp.einsum('bhqk,bhkd->bhqd', p, v)


def mlstm_zero_state(batch):
    return (jnp.zeros((batch, N_HEADS_B, HEAD_DIM, HEAD_DIM), F32),
            jnp.zeros((batch, N_HEADS_B, HEAD_DIM), F32),
            jnp.zeros((batch, N_HEADS_B), F32))


def mlstm_chunk_states(k, v, ig, lf, init):
    b, h, t_len, d = k.shape
    nc = t_len // MLSTM_CHUNK
    kc = k.reshape(b, h, nc, MLSTM_CHUNK, d).astype(F32)
    vc = v.reshape(b, h, nc, MLSTM_CHUNK, d).astype(F32)
    cum = jnp.cumsum(lf.reshape(b, h, nc, MLSTM_CHUNK), axis=-1)
    a = cum[..., -1:] - cum + ig.reshape(b, h, nc, MLSTM_CHUNK)
    m_loc = jnp.max(a, axis=-1)
    w = jnp.exp(a - m_loc[..., None])
    c_loc = jnp.einsum('bhcs,bhcsk,bhcsv->bhckv', w, kc, vc)
    n_loc = jnp.einsum('bhcs,bhcsk->bhck', w, kc)
    b_tot = cum[..., -1]

    def step(carry, xs):
        c_st, n_st, m_st = carry
        cl, nl, ml, bt = xs
        m_new = jnp.maximum(bt + m_st, ml)
        f_old = jnp.exp(bt + m_st - m_new)
        f_loc = jnp.exp(ml - m_new)
        c_new = f_old[..., None, None] * c_st + f_loc[..., None, None] * cl
        n_new = f_old[..., None] * n_st + f_loc[..., None] * nl
        return (c_new, n_new, m_new), (c_st, n_st, m_st)

    xs = tuple(jnp.moveaxis(t, 2, 0) for t in (c_loc, n_loc, m_loc, b_tot))
    final, prev = lax.scan(step, init, xs)
    prev = tuple(jnp.moveaxis(t, 0, 2) for t in prev)
    return prev, final


def mlstm_chunk_outputs(q, k, v, ig, lf, prev):
    c_prev, n_prev, m_prev = prev
    b, h, t_len, d = q.shape
    nc = t_len // MLSTM_CHUNK
    qc = q.reshape(b, h, nc, MLSTM_CHUNK, d).astype(F32)
    kc = k.reshape(b, h, nc, MLSTM_CHUNK, d).astype(F32)
    vc = v.reshape(b, h, nc, MLSTM_CHUNK, d).astype(F32)
    cum = jnp.cumsum(lf.reshape(b, h, nc, MLSTM_CHUNK), axis=-1)
    igc = ig.reshape(b, h, nc, MLSTM_CHUNK)
    lower = np.tril(np.ones((MLSTM_CHUNK, MLSTM_CHUNK), dtype=bool))
    dmat = jnp.where(lower, cum[..., :, None] - cum[..., None, :] + igc[..., None, :], NEG)
    inter = cum + m_prev[..., None]
    m_t = jnp.maximum(inter, jnp.max(dmat, axis=-1))
    wts = jnp.exp(dmat - m_t[..., None]) * jnp.einsum('bhctd,bhcsd->bhcts', qc, kc)
    g = jnp.exp(inter - m_t)
    num = (jnp.einsum('bhcts,bhcsv->bhctv', wts, vc)
           + g[..., None] * jnp.einsum('bhctk,bhckv->bhctv', qc, c_prev))
    den = jnp.sum(wts, axis=-1) + g * jnp.einsum('bhctk,bhck->bhct', qc, n_prev)
    out = num / jnp.maximum(jnp.abs(den), jnp.exp(-m_t))[..., None]
    return out.reshape(b, h, t_len, d)


def _flip(t):
    return jnp.flip(t, axis=2)


def mlstm_bidir_states(k, v, gates, init_f, init_b):
    ig_f, lf_f, ig_b, lf_b = gates
    prev_f, fin_f = mlstm_chunk_states(k, v, ig_f, lf_f, init_f)
    prev_b, fin_b = mlstm_chunk_states(_flip(k), _flip(v), _flip(ig_b), _flip(lf_b), init_b)
    return prev_f, prev_b, fin_f, fin_b


def mlstm_bidir_outputs(q, k, v, gates, prev_f, prev_b):
    ig_f, lf_f, ig_b, lf_b = gates
    h_f = mlstm_chunk_outputs(q, k, v, ig_f, lf_f, prev_f)
    h_b = mlstm_chunk_outputs(_flip(q), _flip(k), _flip(v), _flip(ig_b), _flip(lf_b), prev_b)
    return h_f + _flip(h_b)


def mlstm_readout(h, o, head_gain):
    h = rms_norm(h, head_gain.reshape(1, N_HEADS_B, 1, HEAD_DIM))
    return merge_heads(h).astype(o.dtype) * jax.nn.sigmoid(o)


def even_project(t, w_in, conv_w, gate_b, qk_gain):
    p = jnp.dot(t, w_in)
    qa, ka, va, qb, kb, vb, ob, g = jnp.split(p, SPLIT_EVEN, axis=-1)
    qa = rms_norm(split_heads(qa, N_HEADS_A), qk_gain[0]) * ATT_SCALE
    ka = rms_norm(split_heads(ka, N_HEADS_A), qk_gain[1])
    va = split_heads(va, N_HEADS_A)
    qk = jax.nn.silu(centred_dwconv(jnp.concatenate([qb, kb], axis=-1), conv_w))
    qb, kb = jnp.split(qk, 2, axis=-1)
    qb = split_heads(qb, N_HEADS_B)
    kb = split_heads(kb, N_HEADS_B) * (HEAD_DIM ** -0.5)
    vb = split_heads(vb, N_HEADS_B)
    g = (g.astype(F32) + gate_b.astype(F32)).transpose(0, 2, 1)
    ig_f, fg_f, ig_b, fg_b = jnp.split(g, 4, axis=1)
    gates = (ig_f, jax.nn.log_sigmoid(fg_f), ig_b, jax.nn.log_sigmoid(fg_b))
    return qa, ka, va, qb, kb, vb, ob, gates


def even_mixer(xm, cm, w_in, conv_w, gate_b, qk_gain, rpb, head_gain, w_out, need_ctx):
    batch = xm.shape[0]
    qa, ka, va, qb, kb, vb, ob, gates = even_project(xm, w_in, conv_w, gate_b, qk_gain)
    qac, kac, vac, qbc, kbc, vbc, obc, gates_c = even_project(cm, w_in, conv_w, gate_b, qk_gain)
    ya = neighbourhood_attention(qa, ka, va, kac, vac, rpb)
    zero = mlstm_zero_state(batch)
    prev_cf, prev_cb, fin_cf, fin_cb = mlstm_bidir_states(kbc, vbc, gates_c, zero, zero)
    prev_f, prev_b, _, _ = mlstm_bidir_states(kb, vb, gates, fin_cf, fin_cb)
    yb = mlstm_readout(mlstm_bidir_outputs(qb, kb, vb, gates, prev_f, prev_b), ob, head_gain)
    y = jnp.dot(jnp.concatenate([merge_heads(ya), yb], axis=-1), w_out)
    if not need_ctx:
        return y, None
    yac = context_attention(qac, kac, vac)
    ybc = mlstm_readout(mlstm_bidir_outputs(qbc, kbc, vbc, gates_c, prev_cf, prev_cb), obc, head_gain)
    yc = jnp.dot(jnp.concatenate([merge_heads(yac), ybc], axis=-1), w_out)
    return y, yc


def sgu_mixer(x, w_in, v_gain, w_s, b_s, w_out):
    b, t_len, _ = x.shape
    nc = t_len // SGU_CHUNK
    u, v = jnp.split(jax.nn.gelu(jnp.dot(x, w_in)), 2, axis=-1)
    v = rms_norm(v, v_gain).reshape(b, nc, SGU_CHUNK, SGU_GROUPS, SGU_WIDTH // SGU_GROUPS)
    s = jnp.einsum('gts,bcsgd->bctgd', w_s, v) + b_s.T[None, None, :, :, None]
    return jnp.dot(u * s.reshape(b, t_len, SGU_WIDTH), w_out)


def hierarchical_moe(t, w_rg, b_rg, w_re, b_re, w_gate, w_up, w_down):
    n_tok = t.shape[0]
    lg = jnp.dot(t, w_rg).astype(F32) + b_rg.astype(F32)
    top_g, g_sel = lax.top_k(lg, 1)
    p_group = jnp.exp(top_g[:, 0] - jax.nn.logsumexp(lg, axis=-1))
    le = (jnp.dot(t, w_re).astype(F32) + b_re.astype(F32)).reshape(n_tok, N_GROUPS, EXPERTS_PER_GROUP)
    le_sel = jnp.take_along_axis(le, g_sel[:, :, None], axis=1)[:, 0]
    top_e, e_sel = lax.top_k(le_sel, TOP_K_IN_GROUP)
    p_exp = jax.nn.softmax(top_e, axis=-1)
    expert_id = g_sel * EXPERTS_PER_GROUP + e_sel
    combine = jnp.einsum('tk,tke->te', p_group[:, None] * p_exp,
                         jax.nn.one_hot(expert_id, N_EXPERTS, dtype=F32))
    out = jnp.zeros(t.shape, F32)
    for g in range(N_GROUPS):
        sl = slice(g * EXPERTS_PER_GROUP, (g + 1) * EXPERTS_PER_GROUP)
        hid = (jax.nn.silu(jnp.einsum('td,edf->etf', t, w_gate[sl]))
               * jnp.einsum('td,edf->etf', t, w_up[sl])
               * combine[:, sl].T[:, :, None].astype(t.dtype))
        out = out + jnp.einsum('etf,efd->td', hid, w_down[sl]).astype(F32)
    return out.astype(t.dtype)


def setup_inputs(seed: int = 0) -> dict:
    key = jax.random.key(seed)
    keys = iter(jax.random.split(key, 40))

    def nrm(shape, scale):
        return jax.random.normal(next(keys), shape, F32) * scale

    d = D_MODEL
    forget_bias = jnp.linspace(3.0, 6.0, N_HEADS_B, dtype=F32)
    inputs = {}
    inputs["x"] = nrm((BATCH, SEQ, d), 1.0)
    inputs["c"] = nrm((BATCH, d), 1.0)
    inputs["ctx"] = nrm((BATCH, CTX_LEN, d), 1.0)
    inputs["c_ctx"] = nrm((d,), 1.0)
    inputs["w_mod"] = nrm((DEPTH, d, 6 * d), 0.5 * d ** -0.5)
    inputs["b_mod"] = nrm((DEPTH, 6 * d), 0.02)
    inputs["norm_gain"] = 1.0 + nrm((DEPTH, 2, d), 0.02)
    inputs["even_w_in"] = nrm((N_EVEN, d, IN_EVEN), d ** -0.5)
    inputs["even_conv"] = nrm((N_EVEN, MLSTM_CONV, 2 * WIDTH_B), MLSTM_CONV ** -0.5)
    inputs["even_gate_b"] = jnp.concatenate([
        nrm((N_EVEN, N_HEADS_B), 0.1), forget_bias + nrm((N_EVEN, N_HEADS_B), 0.1),
        nrm((N_EVEN, N_HEADS_B), 0.1), forget_bias + nrm((N_EVEN, N_HEADS_B), 0.1)], axis=-1)
    inputs["even_qk_gain"] = 1.0 + nrm((N_EVEN, 2, HEAD_DIM), 0.02)
    inputs["even_rpb"] = nrm((N_EVEN, N_HEADS_A, 2 * WIN_ROWS - 1, 2 * WIN_COLS - 1), 0.1)
    inputs["even_head_gain"] = 1.0 + nrm((N_EVEN, WIDTH_B), 0.02)
    inputs["even_w_out"] = nrm((N_EVEN, WIDTH_A + WIDTH_B, d), (WIDTH_A + WIDTH_B) ** -0.5)
    inputs["odd_w_in"] = nrm((N_ODD, d, 2 * SGU_WIDTH), d ** -0.5)
    inputs["odd_v_gain"] = 1.0 + nrm((N_ODD, SGU_WIDTH), 0.02)
    inputs["odd_w_s"] = nrm((N_ODD, SGU_GROUPS, SGU_CHUNK, SGU_CHUNK), SGU_CHUNK ** -0.5)
    inputs["odd_b_s"] = 1.0 + nrm((N_ODD, SGU_GROUPS, SGU_CHUNK), 0.02)
    inputs["odd_w_out"] = nrm((N_ODD, SGU_WIDTH, d), SGU_WIDTH ** -0.5)
    inputs["moe_w_rg"] = nrm((DEPTH, d, N_GROUPS), d ** -0.5)
    inputs["moe_b_rg"] = nrm((DEPTH, N_GROUPS), 0.01)
    inputs["moe_w_re"] = nrm((DEPTH, d, N_EXPERTS), d ** -0.5)
    inputs["moe_b_re"] = nrm((DEPTH, N_EXPERTS), 0.01)
    inputs["moe_w_gate"] = nrm((DEPTH, N_EXPERTS, d, D_EXPERT), d ** -0.5)
    inputs["moe_w_up"] = nrm((DEPTH, N_EXPERTS, d, D_EXPERT), d ** -0.5)
    inputs["moe_w_down"] = nrm((DEPTH, N_EXPERTS, D_EXPERT, d), D_EXPERT ** -0.5)
    return inputs


def reference(x, c, ctx, c_ctx, w_mod, b_mod, norm_gain, even_w_in, even_conv, even_gate_b,
              even_qk_gain, even_rpb, even_head_gain, even_w_out, odd_w_in, odd_v_gain, odd_w_s,
              odd_b_s, odd_w_out, moe_w_rg, moe_b_rg, moe_w_re, moe_b_re, moe_w_gate, moe_w_up,
              moe_w_down):
    b, n, d = x.shape
    n_ctx = ctx.shape[1]
    s_lat = jax.nn.silu(c)
    s_ctx = jax.nn.silu(c_ctx)
    h, hc = x, ctx
    for layer in range(DEPTH):
        i = layer // 2
        is_even = layer % 2 == 0
        need_ctx = any(j % 2 == 0 for j in range(layer + 1, DEPTH))
        use_ctx = need_ctx or is_even
        sh_a, sc_a, g_a, sh_f, sc_f, g_f = jnp.split(
            (jnp.dot(s_lat, w_mod[layer]) + b_mod[layer])[:, None, :], 6, axis=-1)
        xm = modulate(h, norm_gain[layer, 0], sh_a, sc_a)
        if use_ctx:
            csh_a, csc_a, cg_a, csh_f, csc_f, cg_f = jnp.split(
                jnp.dot(s_ctx, w_mod[layer]) + b_mod[layer], 6, axis=-1)
            cm = modulate(hc, norm_gain[layer, 0], csh_a, csc_a)
        if is_even:
            y, yc = even_mixer(xm, cm, even_w_in[i], even_conv[i], even_gate_b[i], even_qk_gain[i],
                               even_rpb[i], even_head_gain[i], even_w_out[i], need_ctx)
        else:
            y = sgu_mixer(xm, odd_w_in[i], odd_v_gain[i], odd_w_s[i], odd_b_s[i], odd_w_out[i])
            if need_ctx:
                yc = sgu_mixer(cm, odd_w_in[i], odd_v_gain[i], odd_w_s[i], odd_b_s[i], odd_w_out[i])
        h = h + g_a * y.astype(h.dtype)
        xm2 = modulate(h, norm_gain[layer, 1], sh_f, sc_f)
        moe_args = (moe_w_rg[layer], moe_b_rg[layer], moe_w_re[layer], moe_b_re[layer],
                    moe_w_gate[layer], moe_w_up[layer], moe_w_down[layer])
        if need_ctx:
            hc = hc + cg_a * yc.astype(hc.dtype)
            cm2 = modulate(hc, norm_gain[layer, 1], csh_f, csc_f)
            tok = jnp.concatenate([xm2.reshape(b * n, d), cm2.reshape(b * n_ctx, d)], axis=0)
            out = hierarchical_moe(tok, *moe_args)
            hc = hc + cg_f * out[b * n:].reshape(b, n_ctx, d)
            y2 = out[:b * n].reshape(b, n, d)
        else:
            y2 = hierarchical_moe(xm2.reshape(b * n, d), *moe_args).reshape(b, n, d)
        h = h + g_f * y2
    return h
```

```python
import functools

import numpy as np
import jax
import jax.numpy as jnp
from jax import lax
from jax.experimental import pallas as pl
from jax.experimental.pallas import tpu as pltpu

F32 = jnp.float32
BF16 = jnp.bfloat16
U32 = jnp.uint32
I32 = jnp.int32

D = 2048
BATCH = 4
SEQ = 2048
CTX = 256
DEPTH = 4
GRID_W = 64
GRID_ROWS = SEQ // GRID_W
HEAD_DIM = 128
N_HEADS = 8
WIDTH = N_HEADS * HEAD_DIM
N_GATE_COLS = 4 * N_HEADS
MAIN_EVEN = 7 * WIDTH
WIN_ROWS = 8
WIN_COLS = 16
ATT_SCALE = HEAD_DIM ** -0.5
SGU_CHUNK = 128
SGU_GROUPS = 8
SGU_WIDTH = 2 * D
N_EXPERTS = 32
EXPERTS_PER_GROUP = 8
N_GROUPS = 4
D_EXPERT = D // 4
EPS = 1e-6
NEG = -1e30

ROWS_B = CTX + SEQ
T = BATCH * ROWS_B
CTX_ROW = 4

VMEM_LIMIT = 56 * 1024 * 1024

TM = ROWS_B // 2
TN = 512
TR = 256
MLSTM_L = 64
MOE_TM = 256
N_ASSIGN = 2 * T
MOE_TILES = N_ASSIGN // MOE_TM + N_EXPERTS
MOE_SLOTS = MOE_TILES * MOE_TM
PERM_CHUNK = 256


def _cparams(n_axes):
    return pltpu.CompilerParams(dimension_semantics=("arbitrary",) * n_axes,
                                vmem_limit_bytes=VMEM_LIMIT)


def _dot(a, b):
    return jnp.dot(a, b, preferred_element_type=F32)


def _dot_nt(a, b):
    return lax.dot_general(a, b, (((1,), (1,)), ((), ())), preferred_element_type=F32)


def _rms(x, gain):
    return x * lax.rsqrt(jnp.mean(x * x, axis=-1, keepdims=True) + EPS) * gain


def _row_vectors(tbl_ref, row0, n_rows):
    b = row0 // ROWS_B
    r = row0 % ROWS_B + lax.broadcasted_iota(I32, (n_rows, 1), 0)
    return jnp.where(r < CTX, tbl_ref[CTX_ROW:CTX_ROW + 1, :], tbl_ref[pl.ds(b, 1), :])


def _modulate(x, gain, sh_ref, sc_ref, row0):
    n = x.shape[0]
    return _rms(x, gain) * (1.0 + _row_vectors(sc_ref, row0, n)) + _row_vectors(sh_ref, row0, n)


def _mod_kernel(c_ref, w_ref, b_ref, o_ref):
    c = c_ref[...]
    s = (c * jax.nn.sigmoid(c)).astype(BF16)
    o_ref[...] = _dot(s, w_ref[...].astype(BF16)) + b_ref[...]


def _mod_table(cvec, w_mod, b_mod):
    tn = 1024
    return pl.pallas_call(
        _mod_kernel,
        grid=(DEPTH, 6 * D // tn),
        in_specs=[pl.BlockSpec((8, D), lambda l, j: (0, 0)),
                  pl.BlockSpec((None, D, tn), lambda l, j: (l, 0, j)),
                  pl.BlockSpec((None, 1, tn), lambda l, j: (l, 0, j))],
        out_specs=pl.BlockSpec((None, 8, tn), lambda l, j: (l, 0, j)),
        out_shape=jax.ShapeDtypeStruct((DEPTH, 8, 6 * D), F32),
        compiler_params=_cparams(2),
        name="mod_table",
    )(cvec, w_mod, b_mod.reshape(DEPTH, 1, 6 * D))


def _gelu_tanh(x):
    return x * (0.5 * (1.0 + jnp.tanh(np.sqrt(2.0 / np.pi).astype(np.float32) * (x + 0.044715 * (x * x * x)))))


def _proj_kernel(h_ref, gain_ref, sh_ref, sc_ref, w_ref, *rest, act, has_bias):
    if has_bias:
        b_ref, o_ref, xm_ref = rest
    else:
        o_ref, xm_ref = rest
    i = pl.program_id(0)

    @pl.when(pl.program_id(1) == 0)
    def _():
        def body(k, _):
            r0 = pl.multiple_of(k * 128, 128)
            x = h_ref[pl.ds(r0, 128), :]
            xm_ref[pl.ds(r0, 128), :] = _modulate(x, gain_ref[...], sh_ref, sc_ref, i * TM + r0).astype(BF16)
            return 0
        lax.fori_loop(0, TM // 128, body, 0)

    acc = _dot(xm_ref[...], w_ref[...].astype(BF16))
    if has_bias:
        acc = acc + b_ref[...]
    if act == "gelu":
        acc = _gelu_tanh(acc)
    o_ref[...] = acc


def _project(h, gain, mod, layer, mod_col, w, w_idx, n_out, tn, act=None, bias=None):
    in_specs = [pl.BlockSpec((TM, D), lambda i, j: (i, 0)),
                pl.BlockSpec((1, D), lambda i, j: (0, 0)),
                pl.BlockSpec((None, 8, D), lambda i, j: (layer, 0, mod_col)),
                pl.BlockSpec((None, 8, D), lambda i, j: (layer, 0, mod_col + 1)),
                pl.BlockSpec((None, D, tn), lambda i, j: (w_idx, 0, j))]
    args = [h, gain.reshape(1, D), mod, mod, w]
    if bias is not None:
        in_specs.append(pl.BlockSpec((1, tn), lambda i, j: (0, j)))
        args.append(bias)
    return pl.pallas_call(
        functools.partial(_proj_kernel, act=act, has_bias=bias is not None),
        grid=(T // TM, n_out // tn),
        in_specs=in_specs,
        out_specs=pl.BlockSpec((TM, tn), lambda i, j: (i, j)),
        out_shape=jax.ShapeDtypeStruct((T, n_out), F32),
        scratch_shapes=[pltpu.VMEM((TM, D), BF16)],
        compiler_params=_cparams(2),
        name="norm_project",
    )(*args)


def _out_kernel(*refs, n_y, tn):
    y_refs = refs[:n_y]
    w_ref, h_ref, g_ref, o_ref = refs[n_y:]
    i = pl.program_id(0)
    acc = None
    k0 = 0
    for y_ref in y_refs:
        kw = y_ref.shape[1]
        part = _dot(y_ref[...], w_ref[k0:k0 + kw, :].astype(BF16))
        acc = part if acc is None else acc + part
        k0 += kw
    o_ref[...] = h_ref[...] + _row_vectors(g_ref, i * TM, TM) * acc


def _out_project(ys, w, w_idx, h, mod, layer, gate_col, tn):
    k_total = sum(y.shape[1] for y in ys)
    in_specs = [pl.BlockSpec((TM, y.shape[1]), lambda i, j: (i, 0)) for y in ys]
    in_specs += [pl.BlockSpec((None, k_total, tn), lambda i, j: (w_idx, 0, j)),
                 pl.BlockSpec((TM, tn), lambda i, j: (i, j)),
                 pl.BlockSpec((None, 8, tn), lambda i, j: (layer, 0, gate_col * (D // tn) + j))]
    return pl.pallas_call(
        functools.partial(_out_kernel, n_y=len(ys), tn=tn),
        grid=(T // TM, D // tn),
        in_specs=in_specs,
        out_specs=pl.BlockSpec((TM, tn), lambda i, j: (i, j)),
        out_shape=jax.ShapeDtypeStruct((T, D), F32),
        compiler_params=_cparams(2),
        name="out_project",
    )(*ys, w, h, mod)


def _bias_table(rpb):
    q = np.arange(GRID_W)
    kc = np.arange(GRID_W)
    w_start = np.clip(q - WIN_COLS // 2, 0, GRID_W - WIN_COLS)
    in_win = (kc[None, :] >= w_start[:, None]) & (kc[None, :] < w_start[:, None] + WIN_COLS)
    dc = np.clip(kc[None, :] - q[:, None] + WIN_COLS - 1, 0, 2 * WIN_COLS - 2)
    dr = np.arange(WIN_ROWS)[:, None] + np.arange(WIN_ROWS)[None, :]
    tbl = rpb[:, dr][:, :, :, dc]
    tbl = jnp.where(in_win[None, None, None], tbl, NEG)
    return tbl.transpose(0, 1, 3, 2, 4).reshape(N_HEADS, WIN_ROWS, GRID_W, WIN_ROWS * GRID_W)


def _softmax_pv(parts):
    m = None
    for s, _ in parts:
        ms = jnp.max(s, axis=-1, keepdims=True)
        m = ms if m is None else jnp.maximum(m, ms)
    l = None
    o = None
    for s, v in parts:
        p = jnp.exp(s - m)
        ls = jnp.sum(p, axis=-1, keepdims=True)
        os_ = _dot(p.astype(BF16), v)
        l = ls if l is None else l + ls
        o = os_ if o is None else o + os_
    return o / l


def _attn_kernel(q_ref, k_ref, v_ref, gain_ref, bias_ref, o_ref, kn_ref, vb_ref):
    r = pl.program_id(2)

    @pl.when(r == 0)
    def _():
        kn_ref[...] = _rms(k_ref[...], gain_ref[1:2, :]).astype(BF16)
        vb_ref[...] = v_ref[...].astype(BF16)

    qb = (_rms(q_ref[...], gain_ref[0:1, :]) * ATT_SCALE).astype(BF16)
    kc = kn_ref[0:CTX, :]
    vc = vb_ref[0:CTX, :]
    s_ctx = _dot_nt(qb, kc)

    @pl.when(r == 0)
    def _():
        o_ref[...] = _softmax_pv([(s_ctx, vc)]).astype(BF16)

    @pl.when(r > 0)
    def _():
        rows_per_tile = TR // GRID_W
        for rr in range(rows_per_tile):
            row = (r - 1) * rows_per_tile + rr
            r0 = jnp.clip(row - WIN_ROWS // 2, 0, GRID_ROWS - WIN_ROWS)
            dr0 = r0 - row + WIN_ROWS - 1
            k0 = pl.multiple_of(CTX + r0 * GRID_W, GRID_W)
            kl = kn_ref[pl.ds(k0, WIN_ROWS * GRID_W), :]
            vl = vb_ref[pl.ds(k0, WIN_ROWS * GRID_W), :]
            sl = slice(rr * GRID_W, (rr + 1) * GRID_W)
            s_loc = _dot_nt(qb[sl], kl) + bias_ref[dr0]
            o_ref[sl, :] = _softmax_pv([(s_loc, vl), (s_ctx[sl], vc)]).astype(BF16)


def _attention(p, qk_gain, bias_tbl):
    tiles_b = ROWS_B // TR
    return pl.pallas_call(
        _attn_kernel,
        grid=(N_HEADS, BATCH, tiles_b),
        in_specs=[pl.BlockSpec((TR, HEAD_DIM), lambda h, b, r: (b * tiles_b + r, h)),
                  pl.BlockSpec((ROWS_B, HEAD_DIM), lambda h, b, r: (b, N_HEADS + h)),
                  pl.BlockSpec((ROWS_B, HEAD_DIM), lambda h, b, r: (b, 2 * N_HEADS + h)),
                  pl.BlockSpec((2, HEAD_DIM), lambda h, b, r: (0, 0)),
                  pl.BlockSpec((None, WIN_ROWS, GRID_W, WIN_ROWS * GRID_W), lambda h, b, r: (h, 0, 0, 0))],
        out_specs=pl.BlockSpec((TR, HEAD_DIM), lambda h, b, r: (b * tiles_b + r, h)),
        out_shape=jax.ShapeDtypeStruct((T, WIDTH), BF16),
        scratch_shapes=[pltpu.VMEM((ROWS_B, HEAD_DIM), BF16), pltpu.VMEM((ROWS_B, HEAD_DIM), BF16)],
        compiler_params=_cparams(3),
        name="nbr_attention",
    )(p, p, p, qk_gain, bias_tbl)


def _mlstm_chunk(q, k, v, ig_col, fg_col, ig_row, fg_row, state, reverse):
    c_st, n_st, m_st = state
    L = q.shape[0]
    ti = lax.broadcasted_iota(I32, (L, L), 0)
    si = lax.broadcasted_iota(I32, (L, L), 1)
    lf_col = jax.nn.log_sigmoid(fg_col)
    lf_row = jax.nn.log_sigmoid(fg_row)
    incl = (si >= ti) if reverse else (si <= ti)
    incl_t = (ti >= si) if reverse else (ti <= si)
    cum_col = jnp.sum(jnp.where(incl, lf_row, 0.0), axis=1, keepdims=True)
    cum_row = jnp.sum(jnp.where(incl_t, lf_col, 0.0), axis=0, keepdims=True)
    dmat = jnp.where(incl, cum_col - cum_row + ig_row, NEG)
    inter = cum_col + m_st
    m_t = jnp.maximum(inter, jnp.max(dmat, axis=1, keepdims=True))
    qb = q.astype(BF16)
    kb = k.astype(BF16)
    vb = v.astype(BF16)
    wts = jnp.exp(dmat - m_t) * _dot_nt(qb, kb)
    g = jnp.exp(inter - m_t)
    num = _dot(wts.astype(BF16), vb) + g * _dot(qb, c_st.astype(BF16))
    den = jnp.sum(wts, axis=1, keepdims=True) + g * jnp.sum(q * n_st, axis=1, keepdims=True)
    h = num / jnp.maximum(jnp.abs(den), jnp.exp(-m_t))

    b_tot = jnp.sum(lf_row, axis=1, keepdims=True)
    a_col = b_tot - cum_col + ig_col
    m_loc = jnp.max(a_col, axis=0, keepdims=True)
    kw = k * jnp.exp(a_col - m_loc)
    c_loc = _dot(kw.T.astype(BF16), vb)
    n_loc = jnp.sum(kw, axis=0, keepdims=True)
    m_new = jnp.maximum(b_tot + m_st, m_loc)
    f_old = jnp.exp(b_tot + m_st - m_new)
    f_loc = jnp.exp(m_loc - m_new)
    return h, (f_old * c_st + f_loc * c_loc, f_old * n_st + f_loc * n_loc, m_new)


def _mlstm_kernel(q_ref, k_ref, v_ref, o_ref, cq_ref, ck_ref, gcol_ref, grow_ref, hg_ref, y_ref,
                  qs_ref, ks_ref, hf_ref, *, L):
    nc = ROWS_B // L
    nc_ctx = CTX // L
    row = lax.broadcasted_iota(I32, (ROWS_B, 1), 0)
    seq_first = (row == 0) | (row == CTX)
    seq_last = (row == CTX - 1) | (row == ROWS_B - 1)

    def conv_silu(x, w):
        x_prev = jnp.where(seq_first, 0.0, pltpu.roll(x, 1, 0))
        x_next = jnp.where(seq_last, 0.0, pltpu.roll(x, ROWS_B - 1, 0))
        y = x_prev * w[0:1, :] + x * w[1:2, :] + x_next * w[2:3, :]
        return y * jax.nn.sigmoid(y)

    qs_ref[...] = conv_silu(q_ref[...], cq_ref[...])
    ks_ref[...] = conv_silu(k_ref[...], ck_ref[...]) * (HEAD_DIM ** -0.5)

    zero = (jnp.zeros((HEAD_DIM, HEAD_DIM), F32), jnp.zeros((1, HEAD_DIM), F32), jnp.zeros((1, 1), F32))

    def run_chunk(c, state, reverse):
        rows = pl.ds(pl.multiple_of(c * L, L), L)
        gc = gcol_ref[c]
        gr = grow_ref[c]
        o = 2 if reverse else 0
        return _mlstm_chunk(qs_ref[rows, :], ks_ref[rows, :], v_ref[rows, :],
                            gc[:, o:o + 1], gc[:, o + 1:o + 2], gr[o:o + 1, :], gr[o + 1:o + 2, :],
                            state, reverse), rows

    def fwd_body(c, state):
        (h, state), rows = run_chunk(c, state, False)
        hf_ref[rows, :] = h
        return state

    lax.fori_loop(0, nc, fwd_body, zero)

    def bwd_body(j, state):
        c = jnp.where(j < nc_ctx, nc_ctx - 1 - j, nc - 1 - (j - nc_ctx))
        (h, state), rows = run_chunk(c, state, True)
        hn = _rms(hf_ref[rows, :] + h, hg_ref[...])
        y_ref[rows, :] = (hn * jax.nn.sigmoid(o_ref[rows, :])).astype(BF16)
        return state

    lax.fori_loop(0, nc, bwd_body, zero)


def _mlstm(p, conv_w, gates, head_gain, L):
    nc = ROWS_B // L
    g5 = gates.reshape(BATCH, nc, L, 4, N_HEADS)
    gcol = g5.transpose(0, 4, 1, 2, 3)
    grow = g5.transpose(0, 4, 1, 3, 2)
    col = lambda k: (lambda b, h: (b, k * N_HEADS + h))
    return pl.pallas_call(
        functools.partial(_mlstm_kernel, L=L),
        grid=(BATCH, N_HEADS),
        in_specs=[pl.BlockSpec((ROWS_B, HEAD_DIM), col(3)),
                  pl.BlockSpec((ROWS_B, HEAD_DIM), col(4)),
                  pl.BlockSpec((ROWS_B, HEAD_DIM), col(5)),
                  pl.BlockSpec((ROWS_B, HEAD_DIM), col(6)),
                  pl.BlockSpec((3, HEAD_DIM), lambda b, h: (0, h)),
                  pl.BlockSpec((3, HEAD_DIM), lambda b, h: (0, N_HEADS + h)),
                  pl.BlockSpec((None, None, nc, L, 4), lambda b, h: (b, h, 0, 0, 0)),
                  pl.BlockSpec((None, None, nc, 4, L), lambda b, h: (b, h, 0, 0, 0)),
                  pl.BlockSpec((1, HEAD_DIM), lambda b, h: (0, h))],
        out_specs=pl.BlockSpec((ROWS_B, HEAD_DIM), lambda b, h: (b, h)),
        out_shape=jax.ShapeDtypeStruct((T, WIDTH), BF16),
        scratch_shapes=[pltpu.VMEM((ROWS_B, HEAD_DIM), F32)] * 3,
        compiler_params=_cparams(2),
        name="mlstm",
    )(p, p, p, p, conv_w, conv_w, gcol, grow, head_gain.reshape(1, WIDTH))


def _sgu_kernel(u_ref, v_ref, gain_ref, ws_ref, bs_ref, z_ref):
    gw = SGU_WIDTH // SGU_GROUPS
    vb = _rms(v_ref[...], gain_ref[...]).astype(BF16)
    for c in range(TR // SGU_CHUNK):
        rows = slice(c * SGU_CHUNK, (c + 1) * SGU_CHUNK)
        for g in range(SGU_GROUPS):
            cols = slice(g * gw, (g + 1) * gw)
            s = _dot(ws_ref[g].astype(BF16), vb[rows, cols]) + bs_ref[:, g:g + 1]
            z_ref[rows, cols] = (u_ref[rows, cols] * s).astype(BF16)


def _sgu(uv, v_gain, w_s, b_s):
    return pl.pallas_call(
        _sgu_kernel,
        grid=(T // TR,),
        in_specs=[pl.BlockSpec((TR, SGU_WIDTH), lambda i: (i, 0)),
                  pl.BlockSpec((TR, SGU_WIDTH), lambda i: (i, 1)),
                  pl.BlockSpec((1, SGU_WIDTH), lambda i: (0, 0)),
                  pl.BlockSpec((SGU_GROUPS, SGU_CHUNK, SGU_CHUNK), lambda i: (0, 0, 0)),
                  pl.BlockSpec((SGU_CHUNK, SGU_GROUPS), lambda i: (0, 0))],
        out_specs=pl.BlockSpec((TR, SGU_WIDTH), lambda i: (i, 0)),
        out_shape=jax.ShapeDtypeStruct((T, SGU_WIDTH), BF16),
        compiler_params=_cparams(1),
        name="sgu",
    )(uv, uv, v_gain.reshape(1, SGU_WIDTH), w_s, b_s.T)


ROUTE_LANES = 128


def _route_kernel(h_ref, gain_ref, sh_ref, sc_ref, wr_ref, br_ref, xp_ref, rt_ref):
    i = pl.program_id(0)
    xb = _modulate(h_ref[...], gain_ref[...], sh_ref, sc_ref, i * TR).astype(BF16)
    lo = lax.bitcast_convert_type(xb[:, :D // 2].astype(F32), U32) >> 16
    hi = lax.bitcast_convert_type(xb[:, D // 2:].astype(F32), U32) & jnp.uint32(0xFFFF0000)
    xp_ref[...] = lo | hi

    logits = _dot(xb, wr_ref[...].astype(BF16)) + br_ref[...]
    lane = lax.broadcasted_iota(I32, logits.shape, 1)
    neg_inf = -jnp.inf

    def top1(vals):
        top = jnp.max(vals, axis=-1, keepdims=True)
        idx = jnp.min(jnp.where(vals == top, lane, ROUTE_LANES), axis=-1, keepdims=True)
        return top, idx

    lg = jnp.where(lane < N_GROUPS, logits, neg_inf)
    top_g, g_sel = top1(lg)
    lse = top_g + jnp.log(jnp.sum(jnp.exp(lg - top_g), axis=-1, keepdims=True))
    p_group = jnp.exp(top_g - lse)
    first = N_GROUPS + g_sel * EXPERTS_PER_GROUP
    le = jnp.where((lane >= first) & (lane < first + EXPERTS_PER_GROUP), logits, neg_inf)
    top_a, lane_a = top1(le)
    top_b, lane_b = top1(jnp.where(lane == lane_a, neg_inf, le))
    e_b = jnp.exp(top_b - top_a)
    denom = 1.0 + e_b
    w_a = p_group * (1.0 / denom)
    w_b = p_group * (e_b / denom)
    out_lane = lax.broadcasted_iota(I32, rt_ref.shape, 1)
    rt = jnp.where(out_lane == 0, w_a,
                   jnp.where(out_lane == 1, w_b,
                             jnp.where(out_lane == 2, (lane_a - N_GROUPS).astype(F32),
                                       jnp.where(out_lane == 3, (lane_b - N_GROUPS).astype(F32), 0.0))))
    rt_ref[...] = rt


def _route(h, gain, mod, layer, w_router, b_router):
    return pl.pallas_call(
        _route_kernel,
        grid=(T // TR,),
        in_specs=[pl.BlockSpec((TR, D), lambda i: (i, 0)),
                  pl.BlockSpec((1, D), lambda i: (0, 0)),
                  pl.BlockSpec((None, 8, D), lambda i: (layer, 0, 3)),
                  pl.BlockSpec((None, 8, D), lambda i: (layer, 0, 4)),
                  pl.BlockSpec((D, ROUTE_LANES), lambda i: (0, 0)),
                  pl.BlockSpec((1, ROUTE_LANES), lambda i: (0, 0))],
        out_specs=[pl.BlockSpec((TR, D // 2), lambda i: (i, 0)),
                   pl.BlockSpec((TR, 8), lambda i: (i, 0))],
        out_shape=[jax.ShapeDtypeStruct((T, D // 2), U32), jax.ShapeDtypeStruct((T, 8), F32)],
        compiler_params=_cparams(1),
        name="norm_route",
    )(h, gain.reshape(1, D), mod, mod, w_router, b_router)


def _dispatch_kernel(slot_ref, cnt_ref, start_ref, nv_ref, xp_hbm, xs_hbm, sem):
    def row_copy(src_row, dst_row):
        return pltpu.make_async_copy(xp_hbm.at[pl.ds(src_row, 1)], xs_hbm.at[pl.ds(dst_row, 1)], sem)

    def chunk(ci, _):
        base = ci * PERM_CHUNK

        def issue(k, _):
            a = base + k
            row_copy(a // 2, slot_ref[a]).start()
            return 0

        def drain(k, _):
            row_copy(0, 0).wait()
            return 0

        lax.fori_loop(0, PERM_CHUNK, issue, 0)
        lax.fori_loop(0, PERM_CHUNK, drain, 0)
        return 0

    lax.fori_loop(0, N_ASSIGN // PERM_CHUNK, chunk, 0)

    def pad_expert(e, _):
        first = start_ref[e] + cnt_ref[e]
        n_pad = (MOE_TM - cnt_ref[e] % MOE_TM) % MOE_TM

        def issue(k, _):
            row_copy(0, first + k).start()
            return 0

        def drain(k, _):
            row_copy(0, 0).wait()
            return 0

        lax.fori_loop(0, n_pad, issue, 0)
        lax.fori_loop(0, n_pad, drain, 0)
        return 0

    lax.fori_loop(0, N_EXPERTS, pad_expert, 0)

    def tile_copy(j):
        return pltpu.make_async_copy(xp_hbm.at[pl.ds(0, MOE_TM)], xs_hbm.at[pl.ds(j * MOE_TM, MOE_TM)], sem)

    def fill_tile(j, _):
        tile_copy(j).start()
        tile_copy(j).wait()
        return 0

    lax.fori_loop(nv_ref[0], MOE_TILES, fill_tile, 0)


def _dispatch(slot, counts, starts, n_valid, xp):
    return pl.pallas_call(
        _dispatch_kernel,
        grid_spec=pltpu.PrefetchScalarGridSpec(
            num_scalar_prefetch=4,
            grid=(1,),
            in_specs=[pl.BlockSpec(memory_space=pl.ANY)],
            out_specs=pl.BlockSpec(memory_space=pl.ANY),
            scratch_shapes=[pltpu.SemaphoreType.DMA(())]),
        out_shape=jax.ShapeDtypeStruct((MOE_SLOTS, D // 2), U32),
        compiler_params=pltpu.CompilerParams(dimension_semantics=("arbitrary",)),
        name="moe_dispatch",
    )(slot, counts, starts, n_valid, xp)


def _expert_kernel(te_ref, ts_ref, nv_ref, x_ref, wg_ref, wu_ref, wd_ref, y_ref):
    @pl.when(pl.program_id(0) >= nv_ref[0])
    def _():
        y_ref[...] = jnp.zeros(y_ref.shape, F32)

    @pl.when(pl.program_id(0) < nv_ref[0])
    def _():
        xw = x_ref[...]
        lo = lax.bitcast_convert_type(xw << 16, F32).astype(BF16)
        hi = lax.bitcast_convert_type(xw & jnp.uint32(0xFFFF0000), F32).astype(BF16)
        half = D // 2

        def up(w_ref):
            return _dot(lo, w_ref[0:half, :].astype(BF16)) + _dot(hi, w_ref[half:D, :].astype(BF16))

        h1 = up(wg_ref)
        hid = (h1 * jax.nn.sigmoid(h1)) * up(wu_ref)
        y_ref[...] = _dot(hid.astype(BF16), wd_ref[...].astype(BF16))


def _experts(tile_expert, tile_src, n_valid, xs, w_gate, w_up, w_down, layer):
    w_in_spec = pl.BlockSpec((None, None, D, D_EXPERT), lambda j, te, ts, nv: (layer, te[j], 0, 0))
    return pl.pallas_call(
        _expert_kernel,
        grid_spec=pltpu.PrefetchScalarGridSpec(
            num_scalar_prefetch=3,
            grid=(MOE_TILES,),
            in_specs=[pl.BlockSpec((MOE_TM, D // 2), lambda j, te, ts, nv: (ts[j], 0)),
                      w_in_spec, w_in_spec,
                      pl.BlockSpec((None, None, D_EXPERT, D), lambda j, te, ts, nv: (layer, te[j], 0, 0))],
            out_specs=pl.BlockSpec((MOE_TM, D), lambda j, te, ts, nv: (j, 0))),
        out_shape=jax.ShapeDtypeStruct((MOE_SLOTS, D), F32),
        compiler_params=_cparams(1),
        name="moe_experts",
    )(tile_expert, tile_src, n_valid, xs, w_gate, w_up, w_down)


def _combine_kernel(slot_ref, h_ref, rt_ref, g_ref, ys_hbm, o_ref, buf_ref, sem, *, row_of_tile):
    i = pl.program_id(0)
    row0 = row_of_tile(i)

    def row_copy(src_row, k, t):
        return pltpu.make_async_copy(ys_hbm.at[pl.ds(src_row, 1)], buf_ref.at[k, pl.ds(t, 1)], sem)

    def issue(t, _):
        a = 2 * (row0 + t)
        row_copy(slot_ref[a], 0, t).start()
        row_copy(slot_ref[a + 1], 1, t).start()
        return 0

    def drain(t, _):
        row_copy(0, 0, 0).wait()
        row_copy(0, 1, 0).wait()
        return 0

    lax.fori_loop(0, TR, issue, 0)
    lax.fori_loop(0, TR, drain, 0)
    rt = rt_ref[...]
    y = rt[:, 0:1] * buf_ref[0] + rt[:, 1:2] * buf_ref[1]
    o_ref[...] = h_ref[...] + _row_vectors(g_ref, row0, TR) * y


def _combine(slot, h, route, mod, layer, ys, final):
    tiles_b = ROWS_B // TR
    if final:
        lat_tiles = SEQ // TR
        tile_of = lambda i: (i // lat_tiles) * tiles_b + CTX // TR + i % lat_tiles
        n_tiles = BATCH * lat_tiles
    else:
        tile_of = lambda i: i
        n_tiles = T // TR
    row_of_tile = lambda i: tile_of(i) * TR
    return pl.pallas_call(
        functools.partial(_combine_kernel, row_of_tile=row_of_tile),
        grid_spec=pltpu.PrefetchScalarGridSpec(
            num_scalar_prefetch=1,
            grid=(n_tiles,),
            in_specs=[pl.BlockSpec((TR, D), lambda i, s: (tile_of(i), 0)),
                      pl.BlockSpec((TR, 8), lambda i, s: (tile_of(i), 0)),
                      pl.BlockSpec((None, 8, D), lambda i, s: (layer, 0, 5)),
                      pl.BlockSpec(memory_space=pl.ANY)],
            out_specs=pl.BlockSpec((TR, D), lambda i, s: (i, 0)),
            scratch_shapes=[pltpu.VMEM((2, TR, D), F32), pltpu.SemaphoreType.DMA(())]),
        out_shape=jax.ShapeDtypeStruct((n_tiles * TR, D), F32),
        compiler_params=_cparams(1),
        name="moe_combine",
    )(slot, h, route, mod, ys)


def _moe_plan(route):
    e_flat = route[:, 2:4].astype(I32).reshape(N_ASSIGN)
    onehot = (e_flat[:, None] == jnp.arange(N_EXPERTS, dtype=I32)[None, :]).astype(I32)
    csum = jnp.cumsum(onehot, axis=0)
    rank = jnp.sum((csum - onehot) * onehot, axis=1)
    counts = csum[-1]
    tiles_e = (counts + MOE_TM - 1) // MOE_TM
    tile_end = jnp.cumsum(tiles_e)
    starts = (tile_end - tiles_e) * MOE_TM
    slot = starts[e_flat] + rank
    n_valid = tile_end[-1]
    j = jnp.minimum(jnp.arange(MOE_TILES, dtype=I32), n_valid - 1)
    tile_expert = jnp.minimum(jnp.searchsorted(tile_end, j, side="right"), N_EXPERTS - 1).astype(I32)
    return slot.astype(I32), counts.astype(I32), starts.astype(I32), tile_expert, j, n_valid.reshape(1).astype(I32)


def _moe(h, gain, mod, layer, w_router, b_router, w_gate, w_up, w_down, final):
    xp, route = _route(h, gain, mod, layer, w_router, b_router)
    slot, counts, starts, tile_expert, tile_src, n_valid = _moe_plan(route)
    xs = _dispatch(slot, counts, starts, n_valid, xp)
    ys = _experts(tile_expert, tile_src, n_valid, xs, w_gate, w_up, w_down, layer)
    return _combine(slot, h, route, mod, layer, ys, final)


def _even_mixer(h, mod, layer, gain, w_in, idx, conv_w, gate_b, qk_gain, rpb, head_gain, w_out):
    p = _project(h, gain, mod, layer, 0, w_in, idx, MAIN_EVEN, TN)
    w_gates = jnp.pad(w_in[idx, :, MAIN_EVEN:], ((0, 0), (0, 128 - N_GATE_COLS)))[None]
    b_gates = jnp.pad(gate_b, (0, 128 - N_GATE_COLS)).reshape(1, 128)
    gates = _project(h, gain, mod, layer, 0, w_gates, 0, 128, 128, bias=b_gates)[:, :N_GATE_COLS]
    ya = _attention(p, qk_gain, _bias_table(rpb))
    yb = _mlstm(p, conv_w, gates, head_gain, MLSTM_L)
    return _out_project([ya, yb], w_out, idx, h, mod, layer, 2, TN)


def _odd_mixer(h, mod, layer, gain, w_in, idx, v_gain, w_s, b_s, w_out):
    uv = _project(h, gain, mod, layer, 0, w_in, idx, 2 * SGU_WIDTH, TN, act="gelu")
    z = _sgu(uv, v_gain, w_s, b_s)
    return _out_project([z], w_out, idx, h, mod, layer, 2, 256)


def kernel(x, c, ctx, c_ctx, w_mod, b_mod, norm_gain, even_w_in, even_conv, even_gate_b, even_qk_gain, even_rpb,
           even_head_gain, even_w_out, odd_w_in, odd_v_gain, odd_w_s, odd_b_s, odd_w_out, moe_w_rg, moe_b_rg,
           moe_w_re, moe_b_re, moe_w_gate, moe_w_up, moe_w_down):
    h = jnp.concatenate([ctx, x], axis=1).reshape(T, D)
    cvec = jnp.concatenate([c, c_ctx[None, :], jnp.zeros((3, D), F32)], axis=0)
    mod = _mod_table(cvec, w_mod, b_mod)
    pad = ROUTE_LANES - N_GROUPS - N_EXPERTS
    for layer in range(DEPTH):
        i = layer // 2
        if layer % 2 == 0:
            h = _even_mixer(h, mod, layer, norm_gain[layer, 0], even_w_in, i, even_conv[i], even_gate_b[i],
                            even_qk_gain[i], even_rpb[i], even_head_gain[i], even_w_out)
        else:
            h = _odd_mixer(h, mod, layer, norm_gain[layer, 0], odd_w_in, i, odd_v_gain[i], odd_w_s[i], odd_b_s[i],
                           odd_w_out)
        w_router = jnp.pad(jnp.concatenate([moe_w_rg[layer], moe_w_re[layer]], axis=1), ((0, 0), (0, pad)))
        b_router = jnp.pad(jnp.concatenate([moe_b_rg[layer], moe_b_re[layer]]), (0, pad)).reshape(1, ROUTE_LANES)
        h = _moe(h, norm_gain[layer, 1], mod, layer, w_router, b_router, moe_w_gate, moe_w_up, moe_w_down,
                 final=layer == DEPTH - 1)
    return h.reshape(BATCH, SEQ, D)
```

```python
import functools

import numpy as np
import jax
import jax.numpy as jnp
from jax import lax
from jax.experimental import pallas as pl
from jax.experimental.pallas import tpu as pltpu

F32 = jnp.float32
BF16 = jnp.bfloat16
I32 = jnp.int32

D = 2048
BATCH = 4
SEQ = 2048
CTX = 256
DEPTH = 4
GRID_W = 64
GRID_ROWS = SEQ // GRID_W
HEAD_DIM = 128
N_HEADS = 8
WIDTH = N_HEADS * HEAD_DIM
N_GATE_COLS = 4 * N_HEADS
MAIN_EVEN = 7 * WIDTH
WIN_ROWS = 8
WIN_COLS = 16
ATT_SCALE = HEAD_DIM ** -0.5
SGU_CHUNK = 128
SGU_GROUPS = 8
SGU_WIDTH = 2 * D
N_EXPERTS = 32
EXPERTS_PER_GROUP = 8
N_GROUPS = 4
D_EXPERT = D // 4
EPS = 1e-6
NEG = -1e30

ROWS_B = CTX + SEQ
T = BATCH * ROWS_B
CTX_ROW = 4

VMEM_LIMIT = 56 * 1024 * 1024

TM = ROWS_B // 2
TN = 512
TR = 256
MLSTM_L = 128
MOE_TM = 256
N_ASSIGN = 2 * T
MOE_TILES = N_ASSIGN // MOE_TM + N_EXPERTS
MOE_SLOTS = MOE_TILES * MOE_TM


def _cparams(n_axes):
    return pltpu.CompilerParams(dimension_semantics=("arbitrary",) * n_axes,
                                vmem_limit_bytes=VMEM_LIMIT)


def _dot(a, b):
    return jnp.dot(a, b, preferred_element_type=F32)


def _dot_nt(a, b):
    return lax.dot_general(a, b, (((1,), (1,)), ((), ())), preferred_element_type=F32)


def _rms(x, gain):
    return x * lax.rsqrt(jnp.mean(x * x, axis=-1, keepdims=True) + EPS) * gain


def _row_vectors(tbl_ref, row0, n_rows):
    b = row0 // ROWS_B
    r = row0 % ROWS_B + lax.broadcasted_iota(I32, (n_rows, 1), 0)
    return jnp.where(r < CTX, tbl_ref[CTX_ROW:CTX_ROW + 1, :], tbl_ref[pl.ds(b, 1), :])


def _modulate(x, gain, sh_ref, sc_ref, row0):
    n = x.shape[0]
    return _rms(x, gain) * (1.0 + _row_vectors(sc_ref, row0, n)) + _row_vectors(sh_ref, row0, n)


def _mod_kernel(c_ref, w_ref, b_ref, o_ref):
    c = c_ref[...]
    s = (c * jax.nn.sigmoid(c)).astype(BF16)
    o_ref[...] = _dot(s, w_ref[...].astype(BF16)) + b_ref[...]


def _mod_table(cvec, w_mod, b_mod):
    tn = 1024
    return pl.pallas_call(
        _mod_kernel,
        grid=(DEPTH, 6 * D // tn),
        in_specs=[pl.BlockSpec((8, D), lambda l, j: (0, 0)),
                  pl.BlockSpec((None, D, tn), lambda l, j: (l, 0, j)),
                  pl.BlockSpec((None, 1, tn), lambda l, j: (l, 0, j))],
        out_specs=pl.BlockSpec((None, 8, tn), lambda l, j: (l, 0, j)),
        out_shape=jax.ShapeDtypeStruct((DEPTH, 8, 6 * D), F32),
        compiler_params=_cparams(2),
        name="mod_table",
    )(cvec, w_mod, b_mod.reshape(DEPTH, 1, 6 * D))


def _gelu_tanh(x):
    return x * (0.5 * (1.0 + jnp.tanh(np.sqrt(2.0 / np.pi).astype(np.float32) * (x + 0.044715 * (x * x * x)))))


def _proj_kernel(h_ref, gain_ref, sh_ref, sc_ref, w_ref, *rest, act, has_bias):
    if has_bias:
        b_ref, o_ref, xm_ref = rest
    else:
        o_ref, xm_ref = rest
    i = pl.program_id(0)

    @pl.when(pl.program_id(1) == 0)
    def _():
        def body(k, _):
            r0 = pl.multiple_of(k * 128, 128)
            x = h_ref[pl.ds(r0, 128), :]
            xm_ref[pl.ds(r0, 128), :] = _modulate(x, gain_ref[...], sh_ref, sc_ref, i * TM + r0).astype(BF16)
            return 0
        lax.fori_loop(0, TM // 128, body, 0)

    acc = _dot(xm_ref[...], w_ref[...].astype(BF16))
    if has_bias:
        acc = acc + b_ref[...]
    if act == "gelu":
        acc = _gelu_tanh(acc)
    o_ref[...] = acc


def _project(h, gain, mod, layer, mod_col, w, w_idx, n_out, tn, act=None, bias=None):
    in_specs = [pl.BlockSpec((TM, D), lambda i, j: (i, 0)),
                pl.BlockSpec((1, D), lambda i, j: (0, 0)),
                pl.BlockSpec((None, 8, D), lambda i, j: (layer, 0, mod_col)),
                pl.BlockSpec((None, 8, D), lambda i, j: (layer, 0, mod_col + 1)),
                pl.BlockSpec((None, D, tn), lambda i, j: (w_idx, 0, j))]
    args = [h, gain.reshape(1, D), mod, mod, w]
    if bias is not None:
        in_specs.append(pl.BlockSpec((1, tn), lambda i, j: (0, j)))
        args.append(bias)
    return pl.pallas_call(
        functools.partial(_proj_kernel, act=act, has_bias=bias is not None),
        grid=(T // TM, n_out // tn),
        in_specs=in_specs,
        out_specs=pl.BlockSpec((TM, tn), lambda i, j: (i, j)),
        out_shape=jax.ShapeDtypeStruct((T, n_out), F32),
        scratch_shapes=[pltpu.VMEM((TM, D), BF16)],
        compiler_params=_cparams(2),
        name="norm_project",
    )(*args)


def _out_kernel(*refs, n_y, tn):
    y_refs = refs[:n_y]
    w_ref, h_ref, g_ref, o_ref = refs[n_y:]
    i = pl.program_id(0)
    acc = None
    k0 = 0
    for y_ref in y_refs:
        kw = y_ref.shape[1]
        part = _dot(y_ref[...], w_ref[k0:k0 + kw, :].astype(BF16))
        acc = part if acc is None else acc + part
        k0 += kw
    o_ref[...] = h_ref[...] + _row_vectors(g_ref, i * TM, TM) * acc


def _out_project(ys, w, w_idx, h, mod, layer, gate_col, tn):
    k_total = sum(y.shape[1] for y in ys)
    in_specs = [pl.BlockSpec((TM, y.shape[1]), lambda i, j: (i, 0)) for y in ys]
    in_specs += [pl.BlockSpec((None, k_total, tn), lambda i, j: (w_idx, 0, j)),
                 pl.BlockSpec((TM, tn), lambda i, j: (i, j)),
                 pl.BlockSpec((None, 8, tn), lambda i, j: (layer, 0, gate_col * (D // tn) + j))]
    return pl.pallas_call(
        functools.partial(_out_kernel, n_y=len(ys), tn=tn),
        grid=(T // TM, D // tn),
        in_specs=in_specs,
        out_specs=pl.BlockSpec((TM, tn), lambda i, j: (i, j)),
        out_shape=jax.ShapeDtypeStruct((T, D), F32),
        compiler_params=_cparams(2),
        name="out_project",
    )(*ys, w, h, mod)


def _bias_table(rpb):
    q = np.arange(GRID_W)
    kc = np.arange(GRID_W)
    w_start = np.clip(q - WIN_COLS // 2, 0, GRID_W - WIN_COLS)
    in_win = (kc[None, :] >= w_start[:, None]) & (kc[None, :] < w_start[:, None] + WIN_COLS)
    dc = np.clip(kc[None, :] - q[:, None] + WIN_COLS - 1, 0, 2 * WIN_COLS - 2)
    dr = np.arange(WIN_ROWS)[:, None] + np.arange(WIN_ROWS)[None, :]
    tbl = rpb[:, dr][:, :, :, dc]
    tbl = jnp.where(in_win[None, None, None], tbl, NEG)
    return tbl.transpose(0, 1, 3, 2, 4).reshape(N_HEADS, WIN_ROWS, GRID_W, WIN_ROWS * GRID_W)


def _softmax_pv(parts):
    m = None
    for s, _ in parts:
        ms = jnp.max(s, axis=-1, keepdims=True)
        m = ms if m is None else jnp.maximum(m, ms)
    l = None
    o = None
    for s, v in parts:
        p = jnp.exp(s - m)
        ls = jnp.sum(p, axis=-1, keepdims=True)
        os_ = _dot(p.astype(BF16), v)
        l = ls if l is None else l + ls
        o = os_ if o is None else o + os_
    return o / l


def _attn_kernel(q_ref, k_ref, v_ref, gain_ref, bias_ref, o_ref, kn_ref, vb_ref):
    r = pl.program_id(2)

    @pl.when(r == 0)
    def _():
        kn_ref[...] = _rms(k_ref[...], gain_ref[1:2, :]).astype(BF16)
        vb_ref[...] = v_ref[...].astype(BF16)

    qb = (_rms(q_ref[...], gain_ref[0:1, :]) * ATT_SCALE).astype(BF16)
    kc = kn_ref[0:CTX, :]
    vc = vb_ref[0:CTX, :]
    s_ctx = _dot_nt(qb, kc)

    @pl.when(r == 0)
    def _():
        o_ref[...] = _softmax_pv([(s_ctx, vc)]).astype(BF16)

    @pl.when(r > 0)
    def _():
        rows_per_tile = TR // GRID_W
        for rr in range(rows_per_tile):
            row = (r - 1) * rows_per_tile + rr
            r0 = jnp.clip(row - WIN_ROWS // 2, 0, GRID_ROWS - WIN_ROWS)
            dr0 = r0 - row + WIN_ROWS - 1
            k0 = pl.multiple_of(CTX + r0 * GRID_W, GRID_W)
            kl = kn_ref[pl.ds(k0, WIN_ROWS * GRID_W), :]
            vl = vb_ref[pl.ds(k0, WIN_ROWS * GRID_W), :]
            sl = slice(rr * GRID_W, (rr + 1) * GRID_W)
            s_loc = _dot_nt(qb[sl], kl) + bias_ref[dr0]
            o_ref[sl, :] = _softmax_pv([(s_loc, vl), (s_ctx[sl], vc)]).astype(BF16)


def _attention(p, qk_gain, bias_tbl):
    tiles_b = ROWS_B // TR
    return pl.pallas_call(
        _attn_kernel,
        grid=(N_HEADS, BATCH, tiles_b),
        in_specs=[pl.BlockSpec((TR, HEAD_DIM), lambda h, b, r: (b * tiles_b + r, h)),
                  pl.BlockSpec((ROWS_B, HEAD_DIM), lambda h, b, r: (b, N_HEADS + h)),
                  pl.BlockSpec((ROWS_B, HEAD_DIM), lambda h, b, r: (b, 2 * N_HEADS + h)),
                  pl.BlockSpec((2, HEAD_DIM), lambda h, b, r: (0, 0)),
                  pl.BlockSpec((None, WIN_ROWS, GRID_W, WIN_ROWS * GRID_W), lambda h, b, r: (h, 0, 0, 0))],
        out_specs=pl.BlockSpec((TR, HEAD_DIM), lambda h, b, r: (b * tiles_b + r, h)),
        out_shape=jax.ShapeDtypeStruct((T, WIDTH), BF16),
        scratch_shapes=[pltpu.VMEM((ROWS_B, HEAD_DIM), BF16), pltpu.VMEM((ROWS_B, HEAD_DIM), BF16)],
        compiler_params=_cparams(3),
        name="nbr_attention",
    )(p, p, p, qk_gain, bias_tbl)


def _mlstm_chunk(q, k, v, ig_col, fg_col, ig_row, fg_row, state, reverse):
    c_st, n_st, m_st = state
    L = q.shape[0]
    ti = lax.broadcasted_iota(I32, (L, L), 0)
    si = lax.broadcasted_iota(I32, (L, L), 1)
    lf_col = jax.nn.log_sigmoid(fg_col)
    lf_row = jax.nn.log_sigmoid(fg_row)
    incl = (si >= ti) if reverse else (si <= ti)
    incl_t = (ti >= si) if reverse else (ti <= si)
    cum_col = jnp.sum(jnp.where(incl, lf_row, 0.0), axis=1, keepdims=True)
    cum_row = jnp.sum(jnp.where(incl_t, lf_col, 0.0), axis=0, keepdims=True)
    dmat = jnp.where(incl, cum_col - cum_row + ig_row, NEG)
    inter = cum_col + m_st
    m_t = jnp.maximum(inter, jnp.max(dmat, axis=1, keepdims=True))
    qb = q.astype(BF16)
    kb = k.astype(BF16)
    vb = v.astype(BF16)
    wts = jnp.exp(dmat - m_t) * _dot_nt(qb, kb)
    g = jnp.exp(inter - m_t)
    num = _dot(wts.astype(BF16), vb) + g * _dot(qb, c_st.astype(BF16))
    den = jnp.sum(wts, axis=1, keepdims=True) + g * jnp.sum(q * n_st, axis=1, keepdims=True)
    h = num / jnp.maximum(jnp.abs(den), jnp.exp(-m_t))

    b_tot = jnp.sum(lf_row, axis=1, keepdims=True)
    a_col = b_tot - cum_col + ig_col
    m_loc = jnp.max(a_col, axis=0, keepdims=True)
    kw = k * jnp.exp(a_col - m_loc)
    c_loc = _dot(kw.T.astype(BF16), vb)
    n_loc = jnp.sum(kw, axis=0, keepdims=True)
    m_new = jnp.maximum(b_tot + m_st, m_loc)
    f_old = jnp.exp(b_tot + m_st - m_new)
    f_loc = jnp.exp(m_loc - m_new)
    return h, (f_old * c_st + f_loc * c_loc, f_old * n_st + f_loc * n_loc, m_new)


def _mlstm_kernel(q_ref, k_ref, v_ref, o_ref, cq_ref, ck_ref, gcol_ref, grow_ref, hg_ref, y_ref,
                  qs_ref, ks_ref, hf_ref, hb_ref, *, L):
    nc = ROWS_B // L
    nc_ctx = CTX // L
    row = lax.broadcasted_iota(I32, (ROWS_B, 1), 0)
    seq_first = (row == 0) | (row == CTX)
    seq_last = (row == CTX - 1) | (row == ROWS_B - 1)

    def conv_silu(x, w):
        x_prev = jnp.where(seq_first, 0.0, pltpu.roll(x, 1, 0))
        x_next = jnp.where(seq_last, 0.0, pltpu.roll(x, ROWS_B - 1, 0))
        y = x_prev * w[0:1, :] + x * w[1:2, :] + x_next * w[2:3, :]
        return y * jax.nn.sigmoid(y)

    qs_ref[...] = conv_silu(q_ref[...], cq_ref[...])
    ks_ref[...] = conv_silu(k_ref[...], ck_ref[...]) * (HEAD_DIM ** -0.5)

    zero = (jnp.zeros((HEAD_DIM, HEAD_DIM), F32), jnp.zeros((1, HEAD_DIM), F32), jnp.zeros((1, 1), F32))

    def run_chunk(c, state, reverse):
        rows = pl.ds(pl.multiple_of(c * L, L), L)
        gc = gcol_ref[c]
        gr = grow_ref[c]
        o = 2 if reverse else 0
        return _mlstm_chunk(qs_ref[rows, :], ks_ref[rows, :], v_ref[rows, :],
                            gc[:, o:o + 1], gc[:, o + 1:o + 2], gr[o:o + 1, :], gr[o + 1:o + 2, :],
                            state, reverse), rows

    def body(j, states):
        s_fwd, s_bwd = states
        (h, s_fwd), rows = run_chunk(j, s_fwd, False)
        hf_ref[rows, :] = h
        c = jnp.where(j < nc_ctx, nc_ctx - 1 - j, nc - 1 - (j - nc_ctx))
        (h, s_bwd), rows = run_chunk(c, s_bwd, True)
        hb_ref[rows, :] = h
        return s_fwd, s_bwd

    lax.fori_loop(0, nc, body, (zero, zero), unroll=2)

    def readout(k, _):
        rows = pl.ds(pl.multiple_of(k * TR, TR), TR)
        hn = _rms(hf_ref[rows, :] + hb_ref[rows, :], hg_ref[...])
        y_ref[rows, :] = (hn * jax.nn.sigmoid(o_ref[rows, :])).astype(BF16)
        return 0

    lax.fori_loop(0, ROWS_B // TR, readout, 0)


def _mlstm(p, conv_w, gates, head_gain, L):
    nc = ROWS_B // L
    g5 = gates.reshape(BATCH, nc, L, 4, N_HEADS)
    gcol = g5.transpose(0, 4, 1, 2, 3)
    grow = g5.transpose(0, 4, 1, 3, 2)
    col = lambda k: (lambda b, h: (b, k * N_HEADS + h))
    return pl.pallas_call(
        functools.partial(_mlstm_kernel, L=L),
        grid=(BATCH, N_HEADS),
        in_specs=[pl.BlockSpec((ROWS_B, HEAD_DIM), col(3)),
                  pl.BlockSpec((ROWS_B, HEAD_DIM), col(4)),
                  pl.BlockSpec((ROWS_B, HEAD_DIM), col(5)),
                  pl.BlockSpec((ROWS_B, HEAD_DIM), col(6)),
                  pl.BlockSpec((3, HEAD_DIM), lambda b, h: (0, h)),
                  pl.BlockSpec((3, HEAD_DIM), lambda b, h: (0, N_HEADS + h)),
                  pl.BlockSpec((None, None, nc, L, 4), lambda b, h: (b, h, 0, 0, 0)),
                  pl.BlockSpec((None, None, nc, 4, L), lambda b, h: (b, h, 0, 0, 0)),
                  pl.BlockSpec((1, HEAD_DIM), lambda b, h: (0, h))],
        out_specs=pl.BlockSpec((ROWS_B, HEAD_DIM), lambda b, h: (b, h)),
        out_shape=jax.ShapeDtypeStruct((T, WIDTH), BF16),
        scratch_shapes=[pltpu.VMEM((ROWS_B, HEAD_DIM), F32)] * 4,
        compiler_params=_cparams(2),
        name="mlstm",
    )(p, p, p, p, conv_w, conv_w, gcol, grow, head_gain.reshape(1, WIDTH))


def _sgu_kernel(u_ref, v_ref, gain_ref, ws_ref, bs_ref, z_ref):
    gw = SGU_WIDTH // SGU_GROUPS
    vb = _rms(v_ref[...], gain_ref[...]).astype(BF16)
    for c in range(TR // SGU_CHUNK):
        rows = slice(c * SGU_CHUNK, (c + 1) * SGU_CHUNK)
        for g in range(SGU_GROUPS):
            cols = slice(g * gw, (g + 1) * gw)
            s = _dot(ws_ref[g].astype(BF16), vb[rows, cols]) + bs_ref[:, g:g + 1]
            z_ref[rows, cols] = (u_ref[rows, cols] * s).astype(BF16)


def _sgu(uv, v_gain, w_s, b_s):
    return pl.pallas_call(
        _sgu_kernel,
        grid=(T // TR,),
        in_specs=[pl.BlockSpec((TR, SGU_WIDTH), lambda i: (i, 0)),
                  pl.BlockSpec((TR, SGU_WIDTH), lambda i: (i, 1)),
                  pl.BlockSpec((1, SGU_WIDTH), lambda i: (0, 0)),
                  pl.BlockSpec((SGU_GROUPS, SGU_CHUNK, SGU_CHUNK), lambda i: (0, 0, 0)),
                  pl.BlockSpec((SGU_CHUNK, SGU_GROUPS), lambda i: (0, 0))],
        out_specs=pl.BlockSpec((TR, SGU_WIDTH), lambda i: (i, 0)),
        out_shape=jax.ShapeDtypeStruct((T, SGU_WIDTH), BF16),
        compiler_params=_cparams(1),
        name="sgu",
    )(uv, uv, v_gain.reshape(1, SGU_WIDTH), w_s, b_s.T)


ROUTE_LANES = 128


def _route_kernel(h_ref, gain_ref, sh_ref, sc_ref, wr_ref, br_ref, xp_ref, rt_ref):
    i = pl.program_id(0)
    xm = _modulate(h_ref[...], gain_ref[...], sh_ref, sc_ref, i * TR)
    xp_ref[...] = xm

    logits = _dot(xm.astype(BF16), wr_ref[...].astype(BF16)) + br_ref[...]
    lane = lax.broadcasted_iota(I32, logits.shape, 1)
    neg_inf = -jnp.inf

    def top1(vals):
        top = jnp.max(vals, axis=-1, keepdims=True)
        idx = jnp.min(jnp.where(vals == top, lane, ROUTE_LANES), axis=-1, keepdims=True)
        return top, idx

    lg = jnp.where(lane < N_GROUPS, logits, neg_inf)
    top_g, g_sel = top1(lg)
    lse = top_g + jnp.log(jnp.sum(jnp.exp(lg - top_g), axis=-1, keepdims=True))
    p_group = jnp.exp(top_g - lse)
    first = N_GROUPS + g_sel * EXPERTS_PER_GROUP
    le = jnp.where((lane >= first) & (lane < first + EXPERTS_PER_GROUP), logits, neg_inf)
    top_a, lane_a = top1(le)
    top_b, lane_b = top1(jnp.where(lane == lane_a, neg_inf, le))
    e_b = jnp.exp(top_b - top_a)
    denom = 1.0 + e_b
    w_a = p_group * (1.0 / denom)
    w_b = p_group * (e_b / denom)
    out_lane = lax.broadcasted_iota(I32, rt_ref.shape, 1)
    rt = jnp.where(out_lane == 0, w_a,
                   jnp.where(out_lane == 1, w_b,
                             jnp.where(out_lane == 2, (lane_a - N_GROUPS).astype(F32),
                                       jnp.where(out_lane == 3, (lane_b - N_GROUPS).astype(F32), 0.0))))
    rt_ref[...] = rt


def _route(h, gain, mod, layer, w_router, b_router):
    return pl.pallas_call(
        _route_kernel,
        grid=(T // TR,),
        in_specs=[pl.BlockSpec((TR, D), lambda i: (i, 0)),
                  pl.BlockSpec((1, D), lambda i: (0, 0)),
                  pl.BlockSpec((None, 8, D), lambda i: (layer, 0, 3)),
                  pl.BlockSpec((None, 8, D), lambda i: (layer, 0, 4)),
                  pl.BlockSpec((D, ROUTE_LANES), lambda i: (0, 0)),
                  pl.BlockSpec((1, ROUTE_LANES), lambda i: (0, 0))],
        out_specs=[pl.BlockSpec((TR, D), lambda i: (i, 0)),
                   pl.BlockSpec((TR, 8), lambda i: (i, 0))],
        out_shape=[jax.ShapeDtypeStruct((T, D), F32), jax.ShapeDtypeStruct((T, 8), F32)],
        compiler_params=_cparams(1),
        name="norm_route",
    )(h, gain.reshape(1, D), mod, mod, w_router, b_router)


def _dispatch_kernel(slot_ref, cnt_ref, start_ref, nv_ref, x_ref, xs_hbm, sem):
    i = pl.program_id(0)

    def row_copy(t, dst_row):
        return pltpu.make_async_copy(x_ref.at[pl.ds(t, 1)], xs_hbm.at[pl.ds(dst_row, 1)], sem)

    def drain_rows(n, unroll=1):
        def drain(k, _):
            row_copy(0, 0).wait()
            return 0
        lax.fori_loop(0, n, drain, 0, unroll=unroll)

    def issue(t, _):
        a = 2 * (i * TR + t)
        row_copy(t, slot_ref[a]).start()
        row_copy(t, slot_ref[a + 1]).start()
        return 0

    lax.fori_loop(0, TR, issue, 0, unroll=8)

    @pl.when(i == 0)
    def _():
        def pad_expert(e, _):
            first = start_ref[e] + cnt_ref[e]
            n_pad = (MOE_TM - cnt_ref[e] % MOE_TM) % MOE_TM

            def pad(k, _):
                row_copy(0, first + k).start()
                return 0

            lax.fori_loop(0, n_pad, pad, 0)
            drain_rows(n_pad)
            return 0

        lax.fori_loop(0, N_EXPERTS, pad_expert, 0)

        def tile_copy(j):
            return pltpu.make_async_copy(x_ref, xs_hbm.at[pl.ds(j * MOE_TM, MOE_TM)], sem)

        def fill_tile(j, _):
            tile_copy(j).start()
            tile_copy(j).wait()
            return 0

        lax.fori_loop(nv_ref[0], MOE_TILES, fill_tile, 0)

    drain_rows(2 * TR, unroll=16)


def _dispatch(slot, counts, starts, n_valid, xp):
    assert TR == MOE_TM
    return pl.pallas_call(
        _dispatch_kernel,
        grid_spec=pltpu.PrefetchScalarGridSpec(
            num_scalar_prefetch=4,
            grid=(T // TR,),
            in_specs=[pl.BlockSpec((TR, D), lambda i, *_: (i, 0))],
            out_specs=pl.BlockSpec(memory_space=pl.ANY),
            scratch_shapes=[pltpu.SemaphoreType.DMA(())]),
        out_shape=jax.ShapeDtypeStruct((MOE_SLOTS, D), F32),
        compiler_params=_cparams(1),
        name="moe_dispatch",
    )(slot, counts, starts, n_valid, xp)


def _expert_kernel(te_ref, ts_ref, nv_ref, x_ref, wg_ref, wu_ref, wd_ref, y_ref):
    @pl.when(pl.program_id(0) >= nv_ref[0])
    def _():
        y_ref[...] = jnp.zeros(y_ref.shape, F32)

    @pl.when(pl.program_id(0) < nv_ref[0])
    def _():
        xb = x_ref[...].astype(BF16)
        h1 = _dot(xb, wg_ref[...].astype(BF16))
        hid = (h1 * jax.nn.sigmoid(h1)) * _dot(xb, wu_ref[...].astype(BF16))
        y_ref[...] = _dot(hid.astype(BF16), wd_ref[...].astype(BF16))


def _experts(tile_expert, tile_src, n_valid, xs, w_gate, w_up, w_down, layer):
    w_in_spec = pl.BlockSpec((None, None, D, D_EXPERT), lambda j, te, ts, nv: (layer, te[j], 0, 0))
    return pl.pallas_call(
        _expert_kernel,
        grid_spec=pltpu.PrefetchScalarGridSpec(
            num_scalar_prefetch=3,
            grid=(MOE_TILES,),
            in_specs=[pl.BlockSpec((MOE_TM, D), lambda j, te, ts, nv: (ts[j], 0)),
                      w_in_spec, w_in_spec,
                      pl.BlockSpec((None, None, D_EXPERT, D), lambda j, te, ts, nv: (layer, te[j], 0, 0))],
            out_specs=pl.BlockSpec((MOE_TM, D), lambda j, te, ts, nv: (j, 0))),
        out_shape=jax.ShapeDtypeStruct((MOE_SLOTS, D), F32),
        compiler_params=_cparams(1),
        name="moe_experts",
    )(tile_expert, tile_src, n_valid, xs, w_gate, w_up, w_down)


def _combine_kernel(slot_ref, h_ref, rt_ref, g_ref, ys_hbm, o_ref, buf_ref, sem, *, row_of_tile, n_tiles):
    i = pl.program_id(0)
    cur = i % 2

    def row_copy(src_row, buf, k, t):
        return pltpu.make_async_copy(ys_hbm.at[pl.ds(src_row, 1)], buf_ref.at[buf, k, pl.ds(t, 1)], sem.at[buf])

    def start_tile(tile, buf):
        row0 = row_of_tile(tile)

        def issue(t, _):
            a = 2 * (row0 + t)
            row_copy(slot_ref[a], buf, 0, t).start()
            row_copy(slot_ref[a + 1], buf, 1, t).start()
            return 0

        lax.fori_loop(0, TR, issue, 0, unroll=8)

    @pl.when(i == 0)
    def _():
        start_tile(0, 0)

    @pl.when(i + 1 < n_tiles)
    def _():
        start_tile(i + 1, 1 - cur)

    def drain(t, _):
        row_copy(0, cur, 0, 0).wait()
        row_copy(0, cur, 1, 0).wait()
        return 0

    lax.fori_loop(0, TR, drain, 0, unroll=8)
    rt = rt_ref[...]
    y = rt[:, 0:1] * buf_ref[cur, 0] + rt[:, 1:2] * buf_ref[cur, 1]
    o_ref[...] = h_ref[...] + _row_vectors(g_ref, row_of_tile(i), TR) * y


def _combine(slot, h, route, mod, layer, ys, final):
    tiles_b = ROWS_B // TR
    if final:
        lat_tiles = SEQ // TR
        tile_of = lambda i: (i // lat_tiles) * tiles_b + CTX // TR + i % lat_tiles
        n_tiles = BATCH * lat_tiles
    else:
        tile_of = lambda i: i
        n_tiles = T // TR
    row_of_tile = lambda i: tile_of(i) * TR
    return pl.pallas_call(
        functools.partial(_combine_kernel, row_of_tile=row_of_tile, n_tiles=n_tiles),
        grid_spec=pltpu.PrefetchScalarGridSpec(
            num_scalar_prefetch=1,
            grid=(n_tiles,),
            in_specs=[pl.BlockSpec((TR, D), lambda i, s: (tile_of(i), 0)),
                      pl.BlockSpec((TR, 8), lambda i, s: (tile_of(i), 0)),
                      pl.BlockSpec((None, 8, D), lambda i, s: (layer, 0, 5)),
                      pl.BlockSpec(memory_space=pl.ANY)],
            out_specs=pl.BlockSpec((TR, D), lambda i, s: (i, 0)),
            scratch_shapes=[pltpu.VMEM((2, 2, TR, D), F32), pltpu.SemaphoreType.DMA((2,))]),
        out_shape=jax.ShapeDtypeStruct((n_tiles * TR, D), F32),
        compiler_params=_cparams(1),
        name="moe_combine",
    )(slot, h, route, mod, ys)


def _moe_plan(route):
    e_flat = route[:, 2:4].astype(I32).reshape(N_ASSIGN)
    onehot = (e_flat[:, None] == jnp.arange(N_EXPERTS, dtype=I32)[None, :]).astype(I32)
    csum = jnp.cumsum(onehot, axis=0)
    rank = jnp.sum((csum - onehot) * onehot, axis=1)
    counts = csum[-1]
    tiles_e = (counts + MOE_TM - 1) // MOE_TM
    tile_end = jnp.cumsum(tiles_e)
    starts = (tile_end - tiles_e) * MOE_TM
    slot = starts[e_flat] + rank
    n_valid = tile_end[-1]
    j = jnp.minimum(jnp.arange(MOE_TILES, dtype=I32), n_valid - 1)
    tile_expert = jnp.minimum(jnp.sum((tile_end[None, :] <= j[:, None]).astype(I32), axis=1), N_EXPERTS - 1)
    return slot.astype(I32), counts.astype(I32), starts.astype(I32), tile_expert, j, n_valid.reshape(1).astype(I32)


def _moe(h, gain, mod, layer, w_router, b_router, w_gate, w_up, w_down, final):
    xp, route = _route(h, gain, mod, layer, w_router, b_router)
    slot, counts, starts, tile_expert, tile_src, n_valid = _moe_plan(route)
    xs = _dispatch(slot, counts, starts, n_valid, xp)
    ys = _experts(tile_expert, tile_src, n_valid, xs, w_gate, w_up, w_down, layer)
    return _combine(slot, h, route, mod, layer, ys, final)


def _even_mixer(h, mod, layer, gain, w_in, idx, conv_w, gate_b, qk_gain, rpb, head_gain, w_out):
    p = _project(h, gain, mod, layer, 0, w_in, idx, MAIN_EVEN, TN)
    w_gates = jnp.pad(w_in[idx, :, MAIN_EVEN:], ((0, 0), (0, 128 - N_GATE_COLS)))[None]
    b_gates = jnp.pad(gate_b, (0, 128 - N_GATE_COLS)).reshape(1, 128)
    gates = _project(h, gain, mod, layer, 0, w_gates, 0, 128, 128, bias=b_gates)[:, :N_GATE_COLS]
    ya = _attention(p, qk_gain, _bias_table(rpb))
    yb = _mlstm(p, conv_w, gates, head_gain, MLSTM_L)
    return _out_project([ya, yb], w_out, idx, h, mod, layer, 2, TN)


def _odd_mixer(h, mod, layer, gain, w_in, idx, v_gain, w_s, b_s, w_out):
    uv = _project(h, gain, mod, layer, 0, w_in, idx, 2 * SGU_WIDTH, TN, act="gelu")
    z = _sgu(uv, v_gain, w_s, b_s)
    return _out_project([z], w_out, idx, h, mod, layer, 2, 256)


def kernel(x, c, ctx, c_ctx, w_mod, b_mod, norm_gain, even_w_in, even_conv, even_gate_b, even_qk_gain, even_rpb,
           even_head_gain, even_w_out, odd_w_in, odd_v_gain, odd_w_s, odd_b_s, odd_w_out, moe_w_rg, moe_b_rg,
           moe_w_re, moe_b_re, moe_w_gate, moe_w_up, moe_w_down):
    h = jnp.concatenate([ctx, x], axis=1).reshape(T, D)
    cvec = jnp.concatenate([c, c_ctx[None, :], jnp.zeros((3, D), F32)], axis=0)
    mod = _mod_table(cvec, w_mod, b_mod)
    pad = ROUTE_LANES - N_GROUPS - N_EXPERTS
    for layer in range(DEPTH):
        i = layer // 2
        if layer % 2 == 0:
            h = _even_mixer(h, mod, layer, norm_gain[layer, 0], even_w_in, i, even_conv[i], even_gate_b[i],
                            even_qk_gain[i], even_rpb[i], even_head_gain[i], even_w_out)
        else:
            h = _odd_mixer(h, mod, layer, norm_gain[layer, 0], odd_w_in, i, odd_v_gain[i], odd_w_s[i], odd_b_s[i],
                           odd_w_out)
        w_router = jnp.pad(jnp.concatenate([moe_w_rg[layer], moe_w_re[layer]], axis=1), ((0, 0), (0, pad)))
        b_router = jnp.pad(jnp.concatenate([moe_b_rg[layer], moe_b_re[layer]]), (0, pad)).reshape(1, ROUTE_LANES)
        h = _moe(h, norm_gain[layer, 1], mod, layer, w_router, b_router, moe_w_gate, moe_w_up, moe_w_down,
                 final=layer == DEPTH - 1)
    return h.reshape(BATCH, SEQ, D)
```
